```python
import math
import jax
import jax.numpy as jnp
from jax import lax
import numpy as np

D_MODEL = 1024
BATCH = 1
SEQ = 16384
DEPTH = 1

D_MIX = D_MODEL
GM_HEADS = 8
GM_WIDTH = D_MIX // 2
GM_HEAD_DIM = GM_WIDTH // GM_HEADS
GM_CHUNK = 128
DN_HEADS = 4
DN_WIDTH = D_MIX - GM_WIDTH
DN_HEAD_DIM = DN_WIDTH // DN_HEADS
DN_CHUNK = 64
CONV_K = 4
COL_GM = 0
COL_Q = COL_GM + 2 * GM_WIDTH
COL_K = COL_Q + DN_WIDTH
COL_V = COL_K + DN_WIDTH
COL_Z = COL_V + DN_WIDTH
COL_A = COL_Z + DN_WIDTH
COL_B = COL_A + DN_HEADS
D_IN = COL_B + DN_HEADS
PEER_HEADS = 8
N_KEYS = 128
N_EXPERTS = N_KEYS * N_KEYS
PEER_QUERY_DIM = 256
PEER_HALF = PEER_QUERY_DIM // 2
PEER_TOPK = 16
PEER_BLOCK = 128
EPS = 1e-6

kernel_name = 'hybrid_gmlp_gdn_peer_adaln'


def rms_norm(x, w):
    xf = x.astype(jnp.float32)
    y = xf * lax.rsqrt(jnp.mean(xf * xf, axis=-1, keepdims=True) + EPS)
    return (y * w.astype(jnp.float32)).astype(x.dtype)


def l2_norm(x):
    return x * lax.rsqrt(jnp.sum(x * x, axis=-1, keepdims=True) + EPS)


def modulate(h, shift, scale):
    return h * (1 + scale[:, None, :]) + shift[:, None, :]


def spatial_gating(u, v, norm_w, ws, bs):
    B, S, _ = u.shape
    v = rms_norm(v, norm_w).reshape(B, S // GM_CHUNK, GM_CHUNK, GM_HEADS, GM_HEAD_DIM)
    causal = jnp.tril(jnp.ones((GM_CHUNK, GM_CHUNK), dtype=bool))
    w = jnp.where(causal, ws, 0.0)
    s = jnp.einsum('hts,bnshd->bnthd', w, v) + bs.T[:, :, None]
    return u * s.reshape(B, S, GM_WIDTH)


def short_conv(x, w):
    return lax.conv_general_dilated(
        x, w[:, None, :], window_strides=(1,), padding=[(CONV_K - 1, 0)],
        dimension_numbers=('NWC', 'WIO', 'NWC'), feature_group_count=x.shape[-1])


def gated_delta_rule(q, k, v, g, beta):
    B, S, H, Dk = q.shape
    Dv = v.shape[-1]
    C = DN_CHUNK
    N = S // C

    def chunked(t):
        return t.reshape(B, N, C, H, -1).transpose(0, 3, 1, 2, 4)

    q = chunked(q) * (Dk ** -0.5)
    k = chunked(k)
    v = chunked(v)
    g = g.reshape(B, N, C, H).transpose(0, 3, 1, 2)
    beta = beta.reshape(B, N, C, H).transpose(0, 3, 1, 2)
    gc = jnp.cumsum(g, axis=-1)
    incl = jnp.tril(jnp.ones((C, C), dtype=bool))
    strict = jnp.tril(jnp.ones((C, C), dtype=bool), -1)
    decay = jnp.exp(jnp.where(incl, gc[..., :, None] - gc[..., None, :], -jnp.inf))
    kb = k * beta[..., None]
    kk = jnp.einsum('bhnid,bhnjd->bhnij', kb, k) * decay
    a_mat = jnp.eye(C, dtype=jnp.float32) + jnp.where(strict, kk, 0.0)
    rhs = jnp.concatenate([v * beta[..., None], kb * jnp.exp(gc)[..., None]], axis=-1)
    sol = lax.linalg.triangular_solve(a_mat, rhs, left_side=True, lower=True, unit_diagonal=True)
    u_val, w = sol[..., :Dv], sol[..., Dv:]
    attn = jnp.einsum('bhnid,bhnjd->bhnij', q, k) * decay
    q_dec = q * jnp.exp(gc)[..., None]
    g_last = gc[..., -1]
    k_dec = k * jnp.exp(g_last[..., None] - gc)[..., None]

    def step(state, xs):
        q_c, a_c, w_c, u_c, k_c, gl = xs
        v_new = u_c - jnp.einsum('bhck,bhkv->bhcv', w_c, state)
        o = jnp.einsum('bhck,bhkv->bhcv', q_c, state) + jnp.einsum('bhij,bhjv->bhiv', a_c, v_new)
        state = state * jnp.exp(gl)[..., None, None] + jnp.einsum('bhck,bhcv->bhkv', k_c, v_new)
        return state, o

    xs = tuple(jnp.moveaxis(t, 2, 0) for t in (q_dec, attn, w, u_val, k_dec, g_last))
    state0 = jnp.zeros((B, H, Dk, Dv), jnp.float32)
    _, o = lax.scan(step, state0, xs)
    return o.transpose(1, 0, 3, 2, 4).reshape(B, S, H, Dv)


def hybrid_mixer(p, gm_norm_w, gm_ws, gm_bs, conv_w, a_log, dt_bias, o_norm_w):
    B, S, _ = p.shape
    uv = jax.nn.gelu(p[..., COL_GM:COL_Q])
    y_a = spatial_gating(uv[..., :GM_WIDTH], uv[..., GM_WIDTH:], gm_norm_w, gm_ws, gm_bs)
    qkv = jax.nn.silu(short_conv(p[..., COL_Q:COL_Z], conv_w)).astype(jnp.float32)
    q = l2_norm(qkv[..., :DN_WIDTH].reshape(B, S, DN_HEADS, DN_HEAD_DIM))
    k = l2_norm(qkv[..., DN_WIDTH:2 * DN_WIDTH].reshape(B, S, DN_HEADS, DN_HEAD_DIM))
    v = qkv[..., 2 * DN_WIDTH:].reshape(B, S, DN_HEADS, DN_HEAD_DIM)
    z = p[..., COL_Z:COL_A].astype(jnp.float32).reshape(B, S, DN_HEADS, DN_HEAD_DIM)
    a = p[..., COL_A:COL_B].astype(jnp.float32)
    b = p[..., COL_B:D_IN].astype(jnp.float32)
    g = -jnp.exp(a_log.astype(jnp.float32)) * jax.nn.softplus(a + dt_bias.astype(jnp.float32))
    beta = jax.nn.sigmoid(b)
    o = gated_delta_rule(q, k, v, g, beta)
    o = rms_norm(o, o_norm_w) * jax.nn.silu(z)
    y_b = o.reshape(B, S, DN_WIDTH).astype(p.dtype)
    return jnp.concatenate([y_a, y_b], axis=-1)


def peer_routing(h, wq, keys1, keys2):
    T = h.shape[0]
    q = (h @ wq).astype(jnp.float32).reshape(T, PEER_HEADS, 2, PEER_HALF)
    s1 = jnp.einsum('thd,hkd->thk', q[:, :, 0], keys1.astype(jnp.float32))
    s2 = jnp.einsum('thd,hkd->thk', q[:, :, 1], keys2.astype(jnp.float32))
    v1, i1 = lax.top_k(s1, PEER_TOPK)
    v2, i2 = lax.top_k(s2, PEER_TOPK)
    cand = (v1[..., :, None] + v2[..., None, :]).reshape(T, PEER_HEADS, PEER_TOPK * PEER_TOPK)
    cand_id = (i1[..., :, None] * N_KEYS + i2[..., None, :]).reshape(T, PEER_HEADS, PEER_TOPK * PEER_TOPK)
    score, pos = lax.top_k(cand, PEER_TOPK)
    ids = jnp.take_along_axis(cand_id, pos, axis=-1)
    gates = jax.nn.softmax(score, axis=-1)
    return ids, gates


def peer_apply(h, ids, gates, expert_u, expert_v):
    T, D = h.shape
    nb = T // PEER_BLOCK

    def block(args):
        hb, ib, gb = args
        act = jax.nn.gelu(jnp.einsum('pd,phkd->phk', hb, expert_u[ib]))
        return jnp.einsum('phk,phkd->pd', gb.astype(hb.dtype) * act, expert_v[ib])

    out = lax.map(block, (h.reshape(nb, PEER_BLOCK, D),
                          ids.reshape(nb, PEER_BLOCK, PEER_HEADS, PEER_TOPK),
                          gates.reshape(nb, PEER_BLOCK, PEER_HEADS, PEER_TOPK)))
    return out.reshape(T, D)


def setup_inputs(seed: int = 0) -> dict:
    key = jax.random.key(seed)
    ks = jax.random.split(key, 21)
    f32 = jnp.float32
    L, D = DEPTH, D_MODEL
    dt = jnp.exp(jax.random.uniform(ks[11], (L, DN_HEADS), f32, math.log(1e-3), math.log(1e-1)))
    return {
        'x': jax.random.normal(ks[0], (BATCH, SEQ, D), f32),
        'c': jax.random.normal(ks[1], (BATCH, D), f32),
        'w_ada': jax.random.normal(ks[2], (L, D, 6 * D), f32) * (0.5 * D ** -0.5),
        'b_ada': jax.random.normal(ks[3], (L, 6 * D), f32) * 0.01,
        'norm1_w': 1.0 + 0.01 * jax.random.normal(ks[4], (L, D), f32),
        'w_in': jax.random.normal(ks[5], (L, D, D_IN), f32) * D ** -0.5,
        'gm_norm_w': 1.0 + 0.01 * jax.random.normal(ks[6], (L, GM_WIDTH), f32),
        'gm_ws': jax.random.normal(ks[7], (L, GM_HEADS, GM_CHUNK, GM_CHUNK), f32) * GM_CHUNK ** -0.5,
        'gm_bs': 1.0 + 0.01 * jax.random.normal(ks[8], (L, GM_HEADS, GM_CHUNK), f32),
        'conv_w': jax.random.normal(ks[9], (L, CONV_K, 3 * DN_WIDTH), f32) * CONV_K ** -0.5,
        'a_log': jnp.log(jax.random.uniform(ks[10], (L, DN_HEADS), f32, 1.0, 16.0)),
        'dt_bias': dt + jnp.log(-jnp.expm1(-dt)),
        'o_norm_w': 1.0 + 0.01 * jax.random.normal(ks[12], (L, DN_HEAD_DIM), f32),
        'w_out': jax.random.normal(ks[13], (L, D_MIX, D), f32) * D_MIX ** -0.5,
        'norm2_w': 1.0 + 0.01 * jax.random.normal(ks[14], (L, D), f32),
        'peer_wq': jax.random.normal(ks[15], (L, D, PEER_HEADS * PEER_QUERY_DIM), f32) * D ** -0.5,
        'peer_keys1': jax.random.normal(ks[16], (L, PEER_HEADS, N_KEYS, PEER_HALF), f32) * PEER_HALF ** -0.5,
        'peer_keys2': jax.random.normal(ks[17], (L, PEER_HEADS, N_KEYS, PEER_HALF), f32) * PEER_HALF ** -0.5,
        'peer_u': jax.random.normal(ks[18], (L, N_EXPERTS, D), f32) * D ** -0.5,
        'peer_v': jax.random.normal(ks[19], (L, N_EXPERTS, D), f32) * PEER_HEADS ** -0.5,
        'final_norm_w': 1.0 + 0.01 * jax.random.normal(ks[20], (D,), f32),
    }


def reference(x, c, w_ada, b_ada, norm1_w, w_in, gm_norm_w, gm_ws, gm_bs, conv_w, a_log, dt_bias,
              o_norm_w, w_out, norm2_w, peer_wq, peer_keys1, peer_keys2, peer_u, peer_v, final_norm_w):
    B, S, D = x.shape
    for l in range(DEPTH):
        mod = jax.nn.silu(c) @ w_ada[l] + b_ada[l]
        sh1, sc1, gt1, sh2, sc2, gt2 = jnp.split(mod, 6, axis=-1)
        h = modulate(rms_norm(x, norm1_w[l]), sh1, sc1)
        p = h @ w_in[l]
        mix = hybrid_mixer(p, gm_norm_w[l], gm_ws[l], gm_bs[l], conv_w[l], a_log[l], dt_bias[l], o_norm_w[l])
        x = x + gt1[:, None, :] * (mix @ w_out[l])
        h2 = modulate(rms_norm(x, norm2_w[l]), sh2, sc2).reshape(B * S, D)
        ids, gates = peer_routing(h2, peer_wq[l], peer_keys1[l], peer_keys2[l])
        ff = peer_apply(h2, ids, gates, peer_u[l], peer_v[l]).reshape(B, S, D)
        x = x + gt2[:, None, :] * ff
    return rms_norm(x, final_norm_w)
```

```python
import jax
import jax.numpy as jnp
from jax import lax
from jax.experimental import pallas as pl
from jax.experimental.pallas import tpu as pltpu

F32 = jnp.float32
BF16 = jnp.bfloat16
HIGHEST = lax.Precision.HIGHEST

D_MODEL = 1024
GM_HEADS = 8
GM_WIDTH = 512
GM_CHUNK = 128
DN_HEADS = 4
DN_WIDTH = 512
DN_HEAD_DIM = 128
DN_BLOCK = 128
CONV_K = 4
D_IN = 3080
D_IN_PAD = 3200
COL_Q = 1024
COL_Z = 2560
COL_A = 3072
PEER_HEADS = 8
N_KEYS = 128
N_EXPERTS = N_KEYS * N_KEYS
PEER_TOPK = 16
EPS = 1e-6

LANES = 128
SUBLANES = 8
VMEM_LIMIT = 56 * 1024 * 1024

TM_IN = 512
TB_DN = 256
TS_DN = 512
TM_OUT = 512
T_ROUTE = SUBLANES * LANES
T_PEER = 512
E_PEER = 2048
PEER_PARTS = 4

NT_DIMS = (((1,), (1,)), ((), ()))


def _params(n_axes, flags=None):
    return pltpu.CompilerParams(dimension_semantics=("arbitrary",) * n_axes,
                                vmem_limit_bytes=VMEM_LIMIT, flags=flags)


def _rms(x, w):
    return x * lax.rsqrt(jnp.mean(x * x, axis=-1, keepdims=True) + EPS) * w


def _silu(x):
    return x * jax.nn.sigmoid(x)


_GELU_C1 = 0.7978845608028654
_GELU_C2 = _GELU_C1 * 0.044715


def _gelu(x):
    hx = 0.5 * x
    return hx + hx * jnp.tanh(x * (_GELU_C1 + _GELU_C2 * (x * x)))


def _ada_kernel(c_ref, w_ref, b_ref, o_ref):
    a = _silu(c_ref[...])
    o_ref[...] = jnp.dot(a, w_ref[...], preferred_element_type=F32, precision=HIGHEST) + b_ref[...]


def _ada(c, w_ada, b_ada):
    d = c.shape[-1]
    n = w_ada.shape[-1]
    c8 = jnp.broadcast_to(c, (SUBLANES, d))
    bn = 1024
    out = pl.pallas_call(
        _ada_kernel,
        grid=(n // bn,),
        in_specs=[pl.BlockSpec((SUBLANES, d), lambda j: (0, 0)),
                  pl.BlockSpec((d, bn), lambda j: (0, j)),
                  pl.BlockSpec((1, bn), lambda j: (0, j))],
        out_specs=pl.BlockSpec((SUBLANES, bn), lambda j: (0, j)),
        out_shape=jax.ShapeDtypeStruct((SUBLANES, n), F32),
        compiler_params=_params(1),
        name="ada",
    )(c8, w_ada, b_ada.reshape(1, n))
    return out[0:1]


def _inproj_kernel(x_ref, nw_ref, sh_ref, sc_ref, w_ref, gnw_ref, ws_ref, bsf_ref,
                   ya_ref, qkv_ref, z_ref, ab_ref):
    x = x_ref[...]
    h = _rms(x, nw_ref[...]) * (1.0 + sc_ref[...]) + sh_ref[...]
    hb = h.astype(BF16)

    uv = _gelu(jnp.dot(hb, w_ref[:, 0:COL_Q], preferred_element_type=F32))
    u = uv[:, :GM_WIDTH]
    v = uv[:, GM_WIDTH:]
    vnb = _rms(v, gnw_ref[...]).astype(BF16)

    row = lax.broadcasted_iota(jnp.int32, (GM_CHUNK, GM_CHUNK), 0)
    col = lax.broadcasted_iota(jnp.int32, (GM_CHUNK, GM_CHUNK), 1)
    causal = col <= row
    first_half = col < (LANES // 2)
    for pr in range(GM_HEADS // 2):
        w0 = jnp.where(causal, ws_ref[2 * pr], 0.0).astype(BF16)
        w1 = jnp.where(causal, ws_ref[2 * pr + 1], 0.0).astype(BF16)
        cs = slice(pr * LANES, (pr + 1) * LANES)
        bias = bsf_ref[:, cs]
        for c in range(TM_IN // GM_CHUNK):
            rs = slice(c * GM_CHUNK, (c + 1) * GM_CHUNK)
            vb = vnb[rs, cs]
            r0 = jnp.dot(w0, vb, preferred_element_type=F32)
            r1 = jnp.dot(w1, vb, preferred_element_type=F32)
            s = jnp.where(first_half, r0, r1) + bias
            ya_ref[rs, cs] = u[rs, cs] * s

    qkv_ref[...] = jnp.dot(hb, w_ref[:, COL_Q:COL_Z], preferred_element_type=F32)
    z_ref[...] = jnp.dot(hb, w_ref[:, COL_Z:COL_A], preferred_element_type=F32)
    ab_ref[...] = jnp.dot(hb, w_ref[:, COL_A:D_IN_PAD], preferred_element_type=F32)


def _inproj(x, nw, sh, sc, w_in_b, gnw, ws, bsf):
    t, d = x.shape
    row = lambda i: (i, 0)
    fixed = lambda i: (0, 0)
    return pl.pallas_call(
        _inproj_kernel,
        grid=(t // TM_IN,),
        in_specs=[pl.BlockSpec((TM_IN, d), row),
                  pl.BlockSpec((1, d), fixed),
                  pl.BlockSpec((1, d), fixed),
                  pl.BlockSpec((1, d), fixed),
                  pl.BlockSpec((d, D_IN_PAD), fixed),
                  pl.BlockSpec((1, GM_WIDTH), fixed),
                  pl.BlockSpec((GM_HEADS, GM_CHUNK, GM_CHUNK), lambda i: (0, 0, 0)),
                  pl.BlockSpec((GM_CHUNK, GM_WIDTH), fixed)],
        out_specs=[pl.BlockSpec((TM_IN, GM_WIDTH), row),
                   pl.BlockSpec((TM_IN, 3 * DN_WIDTH), row),
                   pl.BlockSpec((TM_IN, DN_WIDTH), row),
                   pl.BlockSpec((TM_IN, LANES), row)],
        out_shape=[jax.ShapeDtypeStruct((t, GM_WIDTH), F32),
                   jax.ShapeDtypeStruct((t, 3 * DN_WIDTH), F32),
                   jax.ShapeDtypeStruct((t, DN_WIDTH), F32),
                   jax.ShapeDtypeStruct((t, LANES), F32)],
        compiler_params=_params(1),
        name="inproj",
    )(x, nw, sh, sc, w_in_b, gnw, ws, bsf)


def _dn_prep_kernel(qkv_ref, halo_ref, ab_ref, cw_ref, prm_ref,
                    u_ref, w_ref, qd_ref, at_ref, kdt_ref, eg_ref):
    i = pl.program_id(0)
    n = DN_BLOCK

    x = qkv_ref[...]
    halo = jnp.where(i > 0, halo_ref[...], 0.0)
    cw = cw_ref[...]
    rid8 = lax.broadcasted_iota(jnp.int32, halo.shape, 0)
    acc = x * cw[CONV_K - 1:CONV_K, :]
    for s in range(1, CONV_K):
        xs = pltpu.roll(x, s, 0)
        hs = pltpu.roll(halo, s, 0)
        top = jnp.where(rid8 < s, hs, xs[0:SUBLANES])
        shifted = jnp.concatenate([top, xs[SUBLANES:]], axis=0)
        acc = acc + shifted * cw[CONV_K - 1 - s:CONV_K - s, :]
    qkv = _silu(acc)

    ab = ab_ref[...]
    xa = ab + prm_ref[1:2, :]
    softplus = jnp.maximum(xa, 0.0) + jnp.log1p(jnp.exp(-jnp.abs(xa)))
    g_all = -jnp.exp(prm_ref[0:1, :]) * softplus
    beta_all = jax.nn.sigmoid(ab)

    row = lax.broadcasted_iota(jnp.int32, (n, n), 0)
    col = lax.broadcasted_iota(jnp.int32, (n, n), 1)
    incl = col <= row
    strict = col < row
    tril16 = jnp.where(incl, 1.0, 0.0).astype(BF16)
    off_masks = []
    b = 1
    while b < n:
        sh = b.bit_length() - 1
        off_masks.append(((row >> (sh + 1)) == (col >> (sh + 1))) & (((row >> sh) & 1) == 1)
                         & (((col >> sh) & 1) == 0))
        b *= 2
    eg_ref[...] = jnp.zeros_like(eg_ref)

    def mm(a, bm):
        return jnp.dot(a, bm, preferred_element_type=F32)

    chains = []
    for blk in range(TB_DN // n):
        rs = slice(blk * n, (blk + 1) * n)
        g_blk = g_all[rs, :]
        g_hi = g_blk.astype(BF16)
        rem = g_blk - g_hi.astype(F32)
        g_mid = rem.astype(BF16)
        g_lo = (rem - g_mid.astype(F32)).astype(BF16)
        gc_all = mm(tril16, g_hi) + mm(tril16, g_mid) + mm(tril16, g_lo)
        gc_rows = gc_all.T

        for h in range(DN_HEADS):
            hs_ = slice(h * DN_HEAD_DIM, (h + 1) * DN_HEAD_DIM)
            qh = qkv[rs, hs_]
            kh = qkv[rs, DN_WIDTH + h * DN_HEAD_DIM:DN_WIDTH + (h + 1) * DN_HEAD_DIM]
            vh = qkv[rs, 2 * DN_WIDTH + h * DN_HEAD_DIM:2 * DN_WIDTH + (h + 1) * DN_HEAD_DIM]
            q = qh * lax.rsqrt(jnp.sum(qh * qh, axis=-1, keepdims=True) + EPS) * (DN_HEAD_DIM ** -0.5)
            k = kh * lax.rsqrt(jnp.sum(kh * kh, axis=-1, keepdims=True) + EPS)
            bb = jnp.broadcast_to(beta_all[rs, DN_HEADS + h:DN_HEADS + h + 1], (n, n))
            gc = jnp.broadcast_to(gc_all[:, h:h + 1], (n, n))
            gcr = jnp.broadcast_to(gc_rows[h:h + 1, :], (n, n))
            decay = jnp.exp(jnp.where(incl, gc - gcr, -jnp.inf))
            kb = k * bb
            k16 = k.astype(BF16)
            kk = lax.dot_general(kb.astype(BF16), k16, NT_DIMS, preferred_element_type=F32)
            eg = jnp.exp(gc)
            gl = gc[n - 1:n, :]
            qd_ref[rs, hs_] = (q * eg).astype(BF16)
            at_ref[rs, hs_] = (lax.dot_general(q.astype(BF16), k16, NT_DIMS, preferred_element_type=F32)
                               * decay).astype(BF16)
            kdt_ref[rs, hs_] = (k * jnp.exp(gl - gc)).T.astype(BF16)
            eg_ref[blk, h:h + 1, :] = jnp.exp(gl)
            chains.append((rs, hs_, jnp.where(strict, kk * decay, 0.0), vh * bb, kb * eg))

    es = [-jnp.where(off_masks[0], lo, 0.0) for _, _, lo, _, _ in chains]
    for off in off_masks[1:]:
        e16s = [e.astype(BF16) for e in es]
        lbs = [jnp.where(off, lo, 0.0) for _, _, lo, _, _ in chains]
        xms = [lb + mm(lb.astype(BF16), e16) for lb, e16 in zip(lbs, e16s)]
        es = [e - xm - mm(e16, xm.astype(BF16)) for e, xm, e16 in zip(es, xms, e16s)]
    for (rs, hs_, _, ru, rw), e in zip(chains, es):
        e16 = e.astype(BF16)
        u_ref[rs, hs_] = ru + mm(e16, ru.astype(BF16))
        w_ref[rs, hs_] = (rw + mm(e16, rw.astype(BF16))).astype(BF16)


def _dn_scan_kernel(u_ref, w_ref, qd_ref, at_ref, kdt_ref, eg_ref, z_ref, onw_ref, yb_ref, s_ref):
    n = DN_BLOCK

    @pl.when(pl.program_id(0) == 0)
    def _():
        s_ref[...] = jnp.zeros_like(s_ref)

    onw = onw_ref[0:1, :]
    states = [s_ref[h] for h in range(DN_HEADS)]
    for blk in range(TS_DN // n):
        rs = slice(blk * n, (blk + 1) * n)
        for h in range(DN_HEADS):
            hs_ = slice(h * DN_HEAD_DIM, (h + 1) * DN_HEAD_DIM)
            s16 = states[h].astype(BF16)
            v_new = u_ref[rs, hs_] - jnp.dot(w_ref[rs, hs_], s16, preferred_element_type=F32)
            vn16 = v_new.astype(BF16)
            o = (jnp.dot(qd_ref[rs, hs_], s16, preferred_element_type=F32)
                 + jnp.dot(at_ref[rs, hs_], vn16, preferred_element_type=F32))
            states[h] = (states[h] * eg_ref[blk, h:h + 1, :]
                         + jnp.dot(kdt_ref[rs, hs_], vn16, preferred_element_type=F32))
            yb_ref[rs, hs_] = _rms(o, onw) * _silu(z_ref[rs, hs_])
    for h in range(DN_HEADS):
        s_ref[h] = states[h]


def _deltanet(qkv, z, ab, cw8, prm, onw8):
    t = qkv.shape[0]
    n = DN_BLOCK
    row = lambda i: (i, 0)
    fixed = lambda i: (0, 0)
    wide = (t, DN_WIDTH)
    u, w, qd, at, kdt, eg = pl.pallas_call(
        _dn_prep_kernel,
        grid=(t // TB_DN,),
        in_specs=[pl.BlockSpec((TB_DN, 3 * DN_WIDTH), row),
                  pl.BlockSpec((SUBLANES, 3 * DN_WIDTH),
                               lambda i: (jnp.maximum(i * (TB_DN // SUBLANES) - 1, 0), 0)),
                  pl.BlockSpec((TB_DN, LANES), row),
                  pl.BlockSpec((SUBLANES, 3 * DN_WIDTH), fixed),
                  pl.BlockSpec((SUBLANES, LANES), fixed)],
        out_specs=[pl.BlockSpec((TB_DN, DN_WIDTH), row)] * 5
                  + [pl.BlockSpec((TB_DN // n, SUBLANES, LANES), lambda i: (i, 0, 0))],
        out_shape=[jax.ShapeDtypeStruct(wide, F32)] + [jax.ShapeDtypeStruct(wide, BF16)] * 4
                  + [jax.ShapeDtypeStruct((t // n, SUBLANES, LANES), F32)],
        compiler_params=_params(1),
        name="dn_prep",
    )(qkv, qkv, ab, cw8, prm)
    tile = pl.BlockSpec((TS_DN, DN_WIDTH), row)
    return pl.pallas_call(
        _dn_scan_kernel,
        grid=(t // TS_DN,),
        in_specs=[tile] * 5
                 + [pl.BlockSpec((TS_DN // n, SUBLANES, LANES), lambda i: (i, 0, 0)),
                    tile,
                    pl.BlockSpec((SUBLANES, DN_HEAD_DIM), fixed)],
        out_specs=tile,
        out_shape=jax.ShapeDtypeStruct(wide, F32),
        scratch_shapes=[pltpu.VMEM((DN_HEADS, DN_HEAD_DIM, DN_HEAD_DIM), F32)],
        compiler_params=_params(1),
        name="dn_scan",
    )(u, w, qd, at, kdt, eg, z, onw8)


def _outproj_kernel(x_ref, ya_ref, yb_ref, wo_ref, gt1_ref, n2w_ref, sh2_ref, sc2_ref, wq_ref, k1_ref, k2_ref,
                    x1_ref, h2_ref, s_ref):
    mix = (jnp.dot(ya_ref[...].astype(BF16), wo_ref[0:GM_WIDTH, :], preferred_element_type=F32)
           + jnp.dot(yb_ref[...].astype(BF16), wo_ref[GM_WIDTH:, :], preferred_element_type=F32))
    x1 = x_ref[...] + gt1_ref[...] * mix
    x1_ref[...] = x1
    h2 = (_rms(x1, n2w_ref[...]) * (1.0 + sc2_ref[...]) + sh2_ref[...]).astype(BF16)
    h2_ref[...] = h2
    q = jnp.dot(h2, wq_ref[...], preferred_element_type=F32).astype(BF16)
    for h in range(PEER_HEADS):
        c0 = h * 2 * N_KEYS
        s_ref[:, c0:c0 + N_KEYS] = lax.dot_general(q[:, c0:c0 + N_KEYS], k1_ref[h], NT_DIMS,
                                                   preferred_element_type=F32)
        s_ref[:, c0 + N_KEYS:c0 + 2 * N_KEYS] = lax.dot_general(q[:, c0 + N_KEYS:c0 + 2 * N_KEYS], k2_ref[h],
                                                                NT_DIMS, preferred_element_type=F32)


def _outproj(x, ya, yb, wo_b, gt1, n2w, sh2, sc2, wq_b, k1_b, k2_b):
    t, d = x.shape
    nq = wq_b.shape[1]
    row = lambda i: (i, 0)
    fixed = lambda i: (0, 0)
    fixed3 = lambda i: (0, 0, 0)
    return pl.pallas_call(
        _outproj_kernel,
        grid=(t // TM_OUT,),
        in_specs=[pl.BlockSpec((TM_OUT, d), row),
                  pl.BlockSpec((TM_OUT, GM_WIDTH), row),
                  pl.BlockSpec((TM_OUT, DN_WIDTH), row),
                  pl.BlockSpec((d, d), fixed),
                  pl.BlockSpec((1, d), fixed),
                  pl.BlockSpec((1, d), fixed),
                  pl.BlockSpec((1, d), fixed),
                  pl.BlockSpec((1, d), fixed),
                  pl.BlockSpec((d, nq), fixed),
                  pl.BlockSpec((PEER_HEADS, N_KEYS, N_KEYS), fixed3),
                  pl.BlockSpec((PEER_HEADS, N_KEYS, N_KEYS), fixed3)],
        out_specs=[pl.BlockSpec((TM_OUT, d), row),
                   pl.BlockSpec((TM_OUT, d), row),
                   pl.BlockSpec((TM_OUT, nq), row)],
        out_shape=[jax.ShapeDtypeStruct((t, d), F32),
                   jax.ShapeDtypeStruct((t, d), BF16),
                   jax.ShapeDtypeStruct((t, nq), F32)],
        compiler_params=_params(1),
        name="outproj",
    )(x, ya, yb, wo_b, gt1, n2w, sh2, sc2, wq_b, k1_b, k2_b)


def _oem_merge(lo, hi, r):
    step = r * 2
    if step < hi - lo:
        yield from _oem_merge(lo, hi, step)
        yield from _oem_merge(lo + r, hi, step)
        for i in range(lo + r, hi - r, step):
            yield (i, i + r)
    else:
        yield (lo, lo + r)


def _oem_sort(lo, hi):
    if hi - lo >= 1:
        mid = lo + (hi - lo) // 2
        yield from _oem_sort(lo, mid)
        yield from _oem_sort(mid + 1, hi)
        yield from _oem_merge(lo, hi, 1)


_SORT16_NET = tuple(_oem_sort(0, PEER_TOPK - 1))


def _cmpx(a, i, j):
    a[i], a[j] = jnp.maximum(a[i], a[j]), jnp.minimum(a[i], a[j])


def _sort16(vals):
    a = list(vals)
    for i, j in _SORT16_NET:
        _cmpx(a, i, j)
    return a


def _merge16(a, b):
    c = [jnp.maximum(a[i], b[PEER_TOPK - 1 - i]) for i in range(PEER_TOPK)]
    d = PEER_TOPK // 2
    while d >= 1:
        for i in range(PEER_TOPK):
            if i & d == 0:
                _cmpx(c, i, i + d)
        d //= 2
    return c


def _top16_of(get):
    run = _sort16([get(i) for i in range(PEER_TOPK)])
    for g in range(1, N_KEYS // PEER_TOPK):
        run = _merge16(run, _sort16([get(PEER_TOPK * g + i) for i in range(PEER_TOPK)]))
    return run


_ROW_LEN = tuple(PEER_TOPK // (a + 1) for a in range(PEER_TOPK))
_RANK_SUM = float(PEER_TOPK * (N_KEYS - PEER_TOPK) + PEER_TOPK * (PEER_TOPK - 1) // 2)


def _count(masks):
    tot = None
    for mk in masks:
        one = jnp.where(mk, 1.0, 0.0)
        tot = one if tot is None else tot + one
    return tot


def _route_kernel(s_ref, e1_ref, l_ref, e2_ref, rb_ref, x1_scr, x2_scr, e1_scr, l_scr, e2_scr, rb_scr):
    n_sub = T_ROUTE // LANES
    vshape = (SUBLANES, LANES)

    def vreg(scr, i):
        return scr[pl.ds(pl.multiple_of(i * SUBLANES, SUBLANES), SUBLANES), :]

    for side, scr in ((0, x1_scr), (1, x2_scr)):
        for sub in range(n_sub):
            blk = s_ref[sub * LANES:(sub + 1) * LANES, side * N_KEYS:(side + 1) * N_KEYS].T
            scr[pl.ds(sub, N_KEYS, stride=SUBLANES), :] = blk

    v1 = _top16_of(lambda i: vreg(x1_scr, i))
    v2 = _top16_of(lambda i: vreg(x2_scr, i))

    cells = [(a, b) for a in range(PEER_TOPK) for b in range(_ROW_LEN[a])]
    cand = {ab: v1[ab[0]] + v2[ab[1]] for ab in cells}
    ninf = jnp.full(vshape, -jnp.inf, F32)
    row0 = [cand[0, b] for b in range(PEER_TOPK)]
    row1 = [cand[1, b] for b in range(_ROW_LEN[1])] + [ninf] * (PEER_TOPK - _ROW_LEN[1])
    mid = [cand[a, b] for a in range(2, 7) for b in range(_ROW_LEN[a])]
    tail = [cand[a, b] for a in range(7, PEER_TOPK) for b in range(_ROW_LEN[a])]
    assert len(mid) == PEER_TOPK and len(tail) <= PEER_TOPK
    tail = tail + [ninf] * (PEER_TOPK - len(tail))
    top = _merge16(_merge16(row0, row1), _merge16(_sort16(mid), _sort16(tail)))
    mx = top[0]
    tau = top[PEER_TOPK - 1]
    z = None
    for k in range(PEER_TOPK):
        ek = jnp.exp(top[k] - mx)
        z = ek if z is None else z + ek
    sel = {ab: cand[ab] >= tau for ab in cells}
    n_sel = _count(sel.values())

    def thresholds(sel_):
        out = []
        for k in range(1, SUBLANES + 1):
            th = None
            for a in range(PEER_TOPK // k):
                ca = jnp.where(sel_[a, k - 1], v1[a], jnp.inf)
                th = ca if th is None else jnp.minimum(th, ca)
            out.append(th)
        return out

    theta = thresholds(sel)
    extra = jnp.maximum(_count(sel[0, b] for b in range(PEER_TOPK)) - float(SUBLANES), 0.0)
    c1 = v1[0] + jnp.log(z)
    v1_last = v1[PEER_TOPK - 1]

    def side1(i, cnt):
        xi = vreg(x1_scr, i)
        li = _count(xi >= th for th in theta) + jnp.where(xi >= v1[0], extra, 0.0)
        e1_scr[pl.ds(pl.multiple_of(i * SUBLANES, SUBLANES), SUBLANES), :] = jnp.exp(xi - c1)
        l_scr[pl.ds(pl.multiple_of(i * SUBLANES, SUBLANES), SUBLANES), :] = li
        return cnt + jnp.where(xi >= v1_last, 1.0, 0.0)

    cnt1 = lax.fori_loop(0, N_KEYS, side1, jnp.zeros(vshape, F32), unroll=4)

    def rank_in(vs, x):
        t1 = vs[7] > x
        t2 = jnp.where(t1, vs[11], vs[3]) > x
        t3 = jnp.where(t1, jnp.where(t2, vs[13], vs[9]), jnp.where(t2, vs[5], vs[1])) > x
        hi = jnp.where(t2, jnp.where(t3, vs[14], vs[12]), jnp.where(t3, vs[10], vs[8]))
        lo = jnp.where(t2, jnp.where(t3, vs[6], vs[4]), jnp.where(t3, vs[2], vs[0]))
        t4 = jnp.where(t1, hi, lo) > x
        rk = (jnp.where(t1, 8.0, 0.0) + jnp.where(t2, 4.0, 0.0)) + (jnp.where(t3, 2.0, 0.0) + jnp.where(t4, 1.0, 0.0))
        return jnp.where(vs[PEER_TOPK - 1] > x, float(PEER_TOPK), rk)

    def side2(j, rsum):
        xj = vreg(x2_scr, j)
        rj = rank_in(v2, xj)
        e2_scr[pl.ds(pl.multiple_of(j * SUBLANES, SUBLANES), SUBLANES), :] = jnp.exp(xj - v2[0])
        rb_scr[pl.ds(pl.multiple_of(j * SUBLANES, SUBLANES), SUBLANES), :] = rj
        return rsum + rj

    rsum2 = lax.fori_loop(0, N_KEYS, side2, jnp.zeros(vshape, F32), unroll=4)

    tied = (cnt1 != float(PEER_TOPK)) | (rsum2 != _RANK_SUM) | (n_sel != float(PEER_TOPK))
    for r in range(PEER_TOPK - 1):
        tied = tied | (v1[r] == v1[r + 1])
    any_tied = jnp.max(jnp.where(tied, 1.0, 0.0))

    @pl.when(any_tied > 0.0)
    def _():
        def exact_ranks(x_scr, vs, out_scr):
            def outer(i, carry):
                xi = vreg(x_scr, i)
                before = jnp.zeros(vshape, F32)
                for ip in range(N_KEYS - 1):
                    lower = jnp.where(ip < i, 1.0, 0.0)
                    before = before + jnp.where(vreg(x_scr, ip) == xi, lower, 0.0)
                rk = jnp.minimum(_count(v > xi for v in vs) + before, float(PEER_TOPK))
                out_scr[pl.ds(pl.multiple_of(i * SUBLANES, SUBLANES), SUBLANES), :] = rk
                return carry
            lax.fori_loop(0, N_KEYS, outer, 0)

        exact_ranks(x2_scr, v2, rb_scr)
        exact_ranks(x1_scr, v1, l_scr)

        sel_x = {}
        for n, ab in enumerate(cells):
            ahead = _count([cand[o] > cand[ab] for o in cells if o != ab]
                           + [cand[o] == cand[ab] for o in cells[:n]])
            sel_x[ab] = ahead < float(PEER_TOPK)
        zx = None
        for ab in cells:
            ek = jnp.where(sel_x[ab], jnp.exp(cand[ab] - cand[0, 0]), 0.0)
            zx = ek if zx is None else zx + ek
        c1x = v1[0] + jnp.log(zx)
        rows_at_least = [_count(sel_x[a, k - 1] for a in range(PEER_TOPK // k)) for k in range(1, PEER_TOPK + 1)]

        def fix1(i, carry):
            sl = pl.ds(pl.multiple_of(i * SUBLANES, SUBLANES), SUBLANES)
            rk = l_scr[sl, :]
            l_scr[sl, :] = _count(rk < ra for ra in rows_at_least)
            e1_scr[sl, :] = jnp.exp(x1_scr[sl, :] - c1x)
            return carry
        lax.fori_loop(0, N_KEYS, fix1, 0)

    for sub in range(n_sub):
        rows = pl.ds(sub, N_KEYS, stride=SUBLANES)
        e1_ref[sub] = e1_scr[rows, :]
        l_ref[sub] = l_scr[rows, :]
        e2_ref[sub] = e2_scr[rows, :].astype(BF16)
        rb_ref[sub] = rb_scr[rows, :].astype(BF16)


def _route(scores):
    t, nq = scores.shape
    nk = PEER_HEADS * N_KEYS
    out = pl.BlockSpec((T_ROUTE // LANES, N_KEYS, LANES), lambda i, h: (i, h, 0))
    dense = pltpu.VMEM((N_KEYS * SUBLANES, LANES), F32)
    return pl.pallas_call(
        _route_kernel,
        grid=(t // T_ROUTE, PEER_HEADS),
        in_specs=[pl.BlockSpec((T_ROUTE, 2 * N_KEYS), lambda i, h: (i, h))],
        out_specs=[out] * 4,
        out_shape=[jax.ShapeDtypeStruct((t // LANES, nk, LANES), F32),
                   jax.ShapeDtypeStruct((t // LANES, nk, LANES), F32),
                   jax.ShapeDtypeStruct((t // LANES, nk, LANES), BF16),
                   jax.ShapeDtypeStruct((t // LANES, nk, LANES), BF16)],
        scratch_shapes=[dense] * 6,
        compiler_params=_params(2),
        name="route",
    )(scores)


def _peer_kernel(h2_ref, e1_ref, l_ref, e2_ref, rb_ref, u_ref, vt_ref, x1_ref, gt2_ref, fnw_ref,
                 out_ref, h2t_scr, acc_scr, at_scr, ga_scr):
    j = pl.program_id(1)
    nj = pl.num_programs(1)
    d = D_MODEL
    part = E_PEER // PEER_PARTS
    keys_per_part = part // N_KEYS
    assert SUBLANES % keys_per_part == 0

    @pl.when(j == 0)
    def _():
        h2 = h2_ref[...].astype(F32)
        for cb in range(d // LANES):
            h2t_scr[cb * LANES:(cb + 1) * LANES, :] = h2[:, cb * LANES:(cb + 1) * LANES].T.astype(BF16)
        acc_scr[...] = jnp.zeros_like(acc_scr)

    n_ts = T_PEER // LANES

    def activations(p):
        res = jnp.dot(u_ref[p * part:(p + 1) * part, :], h2t_scr[...], preferred_element_type=F32)
        for ts in range(n_ts):
            at_scr[p, ts] = res[:, ts * LANES:(ts + 1) * LANES]

    def accumulate(p):
        ga = jnp.concatenate([ga_scr[p, ts] for ts in range(n_ts)], axis=1)
        acc_scr[...] += jnp.dot(vt_ref[:, p * part:(p + 1) * part], ga, preferred_element_type=F32)

    zero16 = jnp.zeros((N_KEYS, LANES), BF16)

    def gated(p):
        first = p * keys_per_part
        tile_row = j * (E_PEER // N_KEYS) + (first // SUBLANES) * SUBLANES
        for ts in range(n_ts):
            rows = [pl.ds(pl.multiple_of(h * N_KEYS + tile_row, SUBLANES), SUBLANES) for h in range(PEER_HEADS)]
            e1_tiles = [e1_ref[ts, rows[h], :] for h in range(PEER_HEADS)]
            l_tiles = [l_ref[ts, rows[h], :] for h in range(PEER_HEADS)]
            for ik in range(keys_per_part):
                r = first % SUBLANES + ik
                es = slice(ik * N_KEYS, (ik + 1) * N_KEYS)
                gate = zero16
                for h in range(PEER_HEADS):
                    e2 = e2_ref[ts, h * N_KEYS:(h + 1) * N_KEYS, :]
                    rb = rb_ref[ts, h * N_KEYS:(h + 1) * N_KEYS, :]
                    e1 = jnp.broadcast_to(e1_tiles[h][r:r + 1, :], (N_KEYS, LANES)).astype(BF16)
                    ln = jnp.broadcast_to(l_tiles[h][r:r + 1, :], (N_KEYS, LANES)).astype(BF16)
                    picked = jnp.minimum(jnp.maximum(ln - rb, zero16), e1)
                    gate = gate + picked * e2
                act = _gelu(at_scr[p, ts, es, :]).astype(BF16)
                ga_scr[p, ts, es, :] = gate * act

    activations(0)
    activations(1)
    for p in range(PEER_PARTS):
        gated(p)
        accumulate(p)
        if p + 2 < PEER_PARTS:
            activations(p + 2)

    @pl.when(j == nj - 1)
    def _():
        gt2 = gt2_ref[...]
        for cb in range(d // LANES):
            cs = slice(cb * LANES, (cb + 1) * LANES)
            out_ref[:, cs] = x1_ref[:, cs] + gt2[:, cs] * acc_scr[cs, :].T
        out_ref[...] = _rms(out_ref[...], fnw_ref[...])


def _peer(h2, e1t, lt, e2t, rbt, u_b, vt_b, x1, gt2, fnw):
    t, d = x1.shape
    ne = u_b.shape[0]
    tok = lambda i, j: (i, 0)
    keyed = pl.BlockSpec((T_PEER // LANES, PEER_HEADS * N_KEYS, LANES), lambda i, j: (i, 0, 0))
    fixed = lambda i, j: (0, 0)
    return pl.pallas_call(
        _peer_kernel,
        grid=(t // T_PEER, ne // E_PEER),
        in_specs=[pl.BlockSpec((T_PEER, d), tok),
                  keyed, keyed, keyed, keyed,
                  pl.BlockSpec((E_PEER, d), lambda i, j: (j, 0)),
                  pl.BlockSpec((d, E_PEER), lambda i, j: (0, j)),
                  pl.BlockSpec((T_PEER, d), tok),
                  pl.BlockSpec((1, d), fixed),
                  pl.BlockSpec((1, d), fixed)],
        out_specs=pl.BlockSpec((T_PEER, d), tok),
        out_shape=jax.ShapeDtypeStruct((t, d), F32),
        scratch_shapes=[pltpu.VMEM((d, T_PEER), BF16),
                        pltpu.VMEM((d, T_PEER), F32),
                        pltpu.VMEM((PEER_PARTS, T_PEER // LANES, E_PEER // PEER_PARTS, LANES), F32),
                        pltpu.VMEM((PEER_PARTS, T_PEER // LANES, E_PEER // PEER_PARTS, LANES), BF16)],
        compiler_params=_params(2),
        name="peer",
    )(h2, e1t, lt, e2t, rbt, u_b, vt_b, x1, gt2, fnw)


def kernel(x, c, w_ada, b_ada, norm1_w, w_in, gm_norm_w, gm_ws, gm_bs, conv_w, a_log, dt_bias, o_norm_w, w_out,
           norm2_w, peer_wq, peer_keys1, peer_keys2, peer_u, peer_v, final_norm_w):
    b, t, d = x.shape
    assert b == 1 and d == D_MODEL and w_ada.shape[0] == 1
    assert all(t % tile == 0 for tile in (TM_IN, TB_DN, TS_DN, TM_OUT, T_ROUTE, T_PEER))
    x2 = x[0]

    mod = _ada(c, w_ada[0], b_ada[0])
    sh1, sc1, gt1, sh2, sc2, gt2 = (mod[:, i * d:(i + 1) * d] for i in range(6))

    w_in_b = jnp.pad(w_in[0], ((0, 0), (0, D_IN_PAD - D_IN))).astype(BF16)
    bsf = jnp.repeat(gm_bs[0].T, GM_WIDTH // GM_HEADS, axis=1)
    cw8 = jnp.pad(conv_w[0], ((0, SUBLANES - CONV_K), (0, 0)))
    prm = jnp.zeros((SUBLANES, LANES), F32)
    prm = prm.at[0, 0:DN_HEADS].set(a_log[0]).at[1, 0:DN_HEADS].set(dt_bias[0])
    onw8 = jnp.broadcast_to(o_norm_w[0][None, :], (SUBLANES, DN_HEAD_DIM))
    wo_b = w_out[0].astype(BF16)
    wq_b = peer_wq[0].astype(BF16)
    k1_b = peer_keys1[0].astype(BF16)
    k2_b = peer_keys2[0].astype(BF16)
    u_b = peer_u[0].astype(BF16)
    vt_b = peer_v[0].T.astype(BF16)

    ya, qkv, z, ab = _inproj(x2, norm1_w, sh1, sc1, w_in_b, gm_norm_w, gm_ws[0], bsf)
    yb = _deltanet(qkv, z, ab, cw8, prm, onw8)
    x1, h2, scores = _outproj(x2, ya, yb, wo_b, gt1, norm2_w, sh2, sc2, wq_b, k1_b, k2_b)
    e1t, lt, e2t, rbt = _route(scores)
    out = _peer(h2, e1t, lt, e2t, rbt, u_b, vt_b, x1, gt2, final_norm_w[None, :])
    return out[None]
```
